```python
import jax, jax.numpy as jnp
from jax import lax
import numpy as np

D_MODEL = 1024
BATCH = 16
SEQ = 2048
DEPTH = 2
DEC_BATCH = 32
DEC_SEQ = 8
PAST_LEN = 16384
PAGE_SIZE = 128

N_A = DEPTH // 2
N_B = DEPTH - N_A
HEAD_DIM = 64
MEM_WIDTH = D_MODEL // 4
MEM_HEADS = MEM_WIDTH // HEAD_DIM
N_MEM = 256
GMLP_WIDTH = D_MODEL - MEM_WIDTH
CHUNK = 128
GMLP_GDIM = 128
GMLP_GROUPS = GMLP_WIDTH // GMLP_GDIM
SB_WIDTH = D_MODEL - MEM_WIDTH
SB_HEADS = SB_WIDTH // HEAD_DIM
Q_BLOCK = 128
SB_BIAS_MIN = -1.0
SB_BIAS_MAX = -8.0
D_FF = ((8 * D_MODEL // 3 + 255) // 256) * 256
CONV_W = 3
EPS = 1e-6

kernel_name = 'yoco_gmlp_stickbreaking_memory_convffn_step'


def rms_norm(x, g):
    xf = x.astype(jnp.float32)
    y = xf * lax.rsqrt(jnp.mean(xf * xf, axis=-1, keepdims=True) + EPS)
    return (y * g.astype(jnp.float32)).astype(x.dtype)


def chunk_spatial_mix(v, w_s, b_s):
    b, t, _ = v.shape
    vc = v.reshape(b, t // CHUNK, CHUNK, GMLP_GROUPS, GMLP_GDIM)
    causal = jnp.tril(jnp.ones((CHUNK, CHUNK), dtype=bool))
    w = jnp.where(causal[None], w_s, jnp.zeros_like(w_s)).astype(v.dtype)
    s = jnp.einsum('gts,bnsgc->bntgc', w, vc) + b_s.T.astype(v.dtype)[None, None, :, :, None]
    return s.reshape(b, t, GMLP_WIDTH)


def gmlp_mixer(uv, w_s, b_s, g_v):
    z = jax.nn.gelu(uv, approximate=True)
    u, v = z[..., :GMLP_WIDTH], z[..., GMLP_WIDTH:]
    v = rms_norm(v, g_v)
    t = v.shape[1]
    t_pad = -(-t // CHUNK) * CHUNK
    vp = jnp.pad(v, ((0, 0), (0, t_pad - t), (0, 0)))
    s = chunk_spatial_mix(vp, w_s, b_s)[:, :t]
    return u * s, v


def memory_kv(mem, w_mem_kv):
    kv = jnp.einsum('bmd,lde->lbme', mem, w_mem_kv)
    shp = kv.shape[:3] + (MEM_HEADS, HEAD_DIM)
    return kv[..., :MEM_WIDTH].reshape(shp), kv[..., MEM_WIDTH:].reshape(shp)


def mem_attention(q, mk, mv):
    b, t, _ = q.shape
    qh = q.reshape(b, t, MEM_HEADS, HEAD_DIM)
    s = jnp.einsum('bthd,bmhd->bhtm', qh, mk.astype(q.dtype)).astype(jnp.float32) * (HEAD_DIM ** -0.5)
    p = jax.nn.softmax(s, axis=-1).astype(q.dtype)
    o = jnp.einsum('bhtm,bmhd->bthd', p, mv.astype(q.dtype))
    return o.reshape(b, t, MEM_WIDTH)


def stick_breaking_block(q, k, v, bias, q_pos, k_pos):
    z = (jnp.einsum('bqhd,bkhd->bhqk', q, k).astype(jnp.float32) * (HEAD_DIM ** -0.5)
         + bias.astype(jnp.float32)[None, :, None, None])
    mask = k_pos[None, :] < q_pos[:, None]
    log_beta = jax.nn.log_sigmoid(z)
    log_1m = jnp.where(mask, jax.nn.log_sigmoid(-z), 0.0)
    incl = lax.cumsum(log_1m, axis=3, reverse=True)
    excl = jnp.concatenate([incl[..., 1:], jnp.zeros_like(incl[..., :1])], axis=-1)
    a = jnp.where(mask, jnp.exp(log_beta + excl), 0.0)
    return jnp.einsum('bhqk,bkhd->bqhd', a.astype(v.dtype), v)


def stick_breaking_sweep(q, k, v, bias, q_pos, k_pos):
    b, t, h, d = q.shape
    if t % Q_BLOCK != 0:
        o = stick_breaking_block(q, k, v, bias, q_pos, k_pos)
    else:
        nb = t // Q_BLOCK
        qb = q.reshape(b, nb, Q_BLOCK, h, d).swapaxes(0, 1)
        o = lax.map(lambda a: stick_breaking_block(a[0], k, v, bias, a[1], k_pos),
                    (qb, q_pos.reshape(nb, Q_BLOCK)))
        o = o.swapaxes(0, 1)
    return o.reshape(b, t, h * d)


def conv_ffn(h, prev, w_up, conv_w, conv_b, w_down):
    up = h @ w_up
    t = up.shape[1]
    xp = jnp.concatenate([prev.astype(up.dtype), up], axis=1)
    c = conv_b
    for j in range(CONV_W):
        c = c + conv_w[j] * xp[:, j:j + t]
    a, g = c[..., :D_FF], c[..., D_FF:]
    y = (jax.nn.gelu(a, approximate=True) * g) @ w_down
    return y, xp[:, -(CONV_W - 1):]


def run_trunk(x, mem_k, mem_v, conv_prev, past_k, past_v, norms, w_in_a, gmlp_g_v, gmlp_w_s,
              gmlp_b_s, w_out_a, kv_norm, w_kv, w_in_b, w_out_b, sb_bias, w_up, conv_w, conv_b,
              w_down):
    b, t, _ = x.shape
    p = 0 if past_k is None else past_k.shape[1]
    v_rows, conv_rows = [], []
    new_k = new_v = keys = vals = None
    for i in range(DEPTH):
        if i == N_A:
            kv = rms_norm(x, kv_norm) @ w_kv
            new_k = kv[..., :SB_WIDTH].reshape(b, t, SB_HEADS, HEAD_DIM)
            new_v = kv[..., SB_WIDTH:].reshape(b, t, SB_HEADS, HEAD_DIM)
            if past_k is None:
                keys, vals = new_k, new_v
            else:
                keys = jnp.concatenate([past_k.astype(new_k.dtype), new_k], axis=1)
                vals = jnp.concatenate([past_v.astype(new_v.dtype), new_v], axis=1)
        h = rms_norm(x, norms[i, 0])
        if i < N_A:
            proj = h @ w_in_a[i]
            g_out, v_r = gmlp_mixer(proj[..., :2 * GMLP_WIDTH], gmlp_w_s[i], gmlp_b_s[i], gmlp_g_v[i])
            v_rows.append(v_r)
            m_out = mem_attention(proj[..., 2 * GMLP_WIDTH:], mem_k[i], mem_v[i])
            mix = jnp.concatenate([g_out, m_out], axis=-1) @ w_out_a[i]
        else:
            j = i - N_A
            proj = h @ w_in_b[j]
            q = proj[..., :SB_WIDTH].reshape(b, t, SB_HEADS, HEAD_DIM)
            sb = stick_breaking_sweep(q, keys, vals, sb_bias[j], p + jnp.arange(t), jnp.arange(p + t))
            m_out = mem_attention(proj[..., SB_WIDTH:], mem_k[i], mem_v[i])
            mix = jnp.concatenate([sb, m_out], axis=-1) @ w_out_b[j]
        x = x + rms_norm(mix, norms[i, 1])
        f, c_r = conv_ffn(rms_norm(x, norms[i, 2]), conv_prev[i], w_up[i], conv_w[i], conv_b[i], w_down[i])
        conv_rows.append(c_r)
        x = x + rms_norm(f, norms[i, 3])
    return x, new_k, new_v, jnp.stack(v_rows), jnp.stack(conv_rows)


def setup_inputs(seed: int = 0) -> dict:
    key = jax.random.key(seed)
    ks = iter(jax.random.split(key, 32))

    def nrm(shape, scale):
        return jax.random.normal(next(ks), shape, jnp.float32) * scale

    d = D_MODEL
    n_pages = PAST_LEN // PAGE_SIZE
    n_used = DEC_BATCH * n_pages
    n_pool = n_used + max(1, n_used // 4)
    perm = jax.random.permutation(next(ks), n_pool)
    page_table = perm[:n_used].reshape(DEC_BATCH, n_pages).astype(jnp.int32)
    sb_bias_base = jnp.linspace(SB_BIAS_MIN, SB_BIAS_MAX, SB_HEADS, dtype=jnp.float32)
    return {
        'x_prompt': nrm((BATCH, SEQ, d), 1.0),
        'x_sample': nrm((DEC_BATCH, DEC_SEQ, d), 1.0),
        'mem_prompt': nrm((BATCH, N_MEM, d), 1.0),
        'cache_k_pages': nrm((n_pool, PAGE_SIZE, SB_HEADS, HEAD_DIM), 1.0),
        'cache_v_pages': nrm((n_pool, PAGE_SIZE, SB_HEADS, HEAD_DIM), 1.0),
        'page_table': page_table,
        'cache_mem_k': nrm((DEPTH, DEC_BATCH, N_MEM, MEM_HEADS, HEAD_DIM), 1.0),
        'cache_mem_v': nrm((DEPTH, DEC_BATCH, N_MEM, MEM_HEADS, HEAD_DIM), 1.0),
        'state_conv': nrm((DEPTH, DEC_BATCH, CONV_W - 1, 2 * D_FF), 1.0),
        'norms': 1.0 + nrm((DEPTH, 4, d), 0.02),
        'w_in_a': nrm((N_A, d, 2 * GMLP_WIDTH + MEM_WIDTH), d ** -0.5),
        'gmlp_g_v': 1.0 + nrm((N_A, GMLP_WIDTH), 0.02),
        'gmlp_w_s': nrm((N_A, GMLP_GROUPS, CHUNK, CHUNK), CHUNK ** -0.5),
        'gmlp_b_s': 1.0 + nrm((N_A, GMLP_GROUPS, CHUNK), 0.02),
        'w_out_a': nrm((N_A, GMLP_WIDTH + MEM_WIDTH, d), (GMLP_WIDTH + MEM_WIDTH) ** -0.5),
        'kv_norm': 1.0 + nrm((d,), 0.02),
        'w_kv': nrm((d, 2 * SB_WIDTH), d ** -0.5),
        'w_in_b': nrm((N_B, d, SB_WIDTH + MEM_WIDTH), d ** -0.5),
        'w_out_b': nrm((N_B, SB_WIDTH + MEM_WIDTH, d), (SB_WIDTH + MEM_WIDTH) ** -0.5),
        'sb_bias': sb_bias_base[None, :] + nrm((N_B, SB_HEADS), 0.02),
        'w_mem_kv': nrm((DEPTH, d, 2 * MEM_WIDTH), d ** -0.5),
        'w_up': nrm((DEPTH, d, 2 * D_FF), d ** -0.5),
        'conv_w': nrm((DEPTH, CONV_W, 2 * D_FF), CONV_W ** -0.5),
        'conv_b': nrm((DEPTH, 2 * D_FF), 0.02),
        'w_down': nrm((DEPTH, D_FF, d), D_FF ** -0.5),
    }


def reference(x_prompt, x_sample, mem_prompt, cache_k_pages, cache_v_pages, page_table,
              cache_mem_k, cache_mem_v, state_conv, norms, w_in_a, gmlp_g_v, gmlp_w_s, gmlp_b_s,
              w_out_a, kv_norm, w_kv, w_in_b, w_out_b, sb_bias, w_mem_kv, w_up, conv_w, conv_b,
              w_down):
    mem_k_prompt, mem_v_prompt = memory_kv(mem_prompt, w_mem_kv)
    conv_zero = jnp.zeros((DEPTH, x_prompt.shape[0], CONV_W - 1, 2 * D_FF), x_prompt.dtype)
    y_prompt, new_k_prompt, new_v_prompt, _, conv_prompt = run_trunk(
        x_prompt, mem_k_prompt, mem_v_prompt, conv_zero, None, None,
        norms, w_in_a, gmlp_g_v, gmlp_w_s, gmlp_b_s, w_out_a, kv_norm, w_kv, w_in_b, w_out_b,
        sb_bias, w_up, conv_w, conv_b, w_down)
    n_seq = page_table.shape[0]
    past_k = cache_k_pages[page_table].reshape(n_seq, -1, SB_HEADS, HEAD_DIM)
    past_v = cache_v_pages[page_table].reshape(n_seq, -1, SB_HEADS, HEAD_DIM)
    y_sample, new_k_sample, new_v_sample, gmlp_v_sample, conv_sample = run_trunk(
        x_sample, cache_mem_k, cache_mem_v, state_conv, past_k, past_v,
        norms, w_in_a, gmlp_g_v, gmlp_w_s, gmlp_b_s, w_out_a, kv_norm, w_kv, w_in_b, w_out_b,
        sb_bias, w_up, conv_w, conv_b, w_down)
    return (y_prompt, y_sample, new_k_prompt, new_v_prompt, new_k_sample, new_v_sample,
            mem_k_prompt, mem_v_prompt, conv_prompt, conv_sample, gmlp_v_sample)
```

```python
import functools

import jax
import jax.numpy as jnp
from jax import lax
from jax.experimental import pallas as pl
from jax.experimental.pallas import tpu as pltpu

D_MODEL = 1024
DEPTH = 2
HEAD_DIM = 64
MEM_WIDTH = D_MODEL // 4
MEM_HEADS = MEM_WIDTH // HEAD_DIM
N_MEM = 256
GMLP_WIDTH = D_MODEL - MEM_WIDTH
CHUNK = 128
GMLP_GROUPS = GMLP_WIDTH // CHUNK
SB_WIDTH = D_MODEL - MEM_WIDTH
SB_HEADS = SB_WIDTH // HEAD_DIM
HEAD_PAIRS = SB_HEADS // 2
D_FF = 2816
CONV_W = 3
EPS = 1e-6
SCALE = HEAD_DIM ** -0.5
PAGE_SIZE = 128
LANES = 128
SUBLANES = 8
NEG_BIG = -1e30

F32 = jnp.float32
BF16 = jnp.bfloat16

MIXER_ROWS = 512
FFN_ROWS = 512
FFN_COLS = D_FF // 2
PAGES_PER_STEP = 8
VMEM_LIMIT = 56 * 1024 * 1024


def _params(*sem):
    return pltpu.CompilerParams(dimension_semantics=sem, vmem_limit_bytes=VMEM_LIMIT)


def _dot(a, b):
    return jnp.dot(a, b, preferred_element_type=F32)


def _dot_nt(a, b):
    return lax.dot_general(a, b, (((1,), (1,)), ((), ())), preferred_element_type=F32)


def _rms(x, g):
    return x * lax.rsqrt(jnp.mean(x * x, axis=-1, keepdims=True) + EPS) * g


def _gelu(x):
    return jax.nn.gelu(x, approximate=True)


def _iota(shape, dim):
    return lax.broadcasted_iota(jnp.int32, shape, dim)


def _full(shape):
    return pl.BlockSpec(shape, lambda *_: (0,) * len(shape))


def _mem_attention(q, mk_ref, mv_ref, nb, rb, mem_t):
    rows = q.shape[0]
    mk = mk_ref[...].astype(BF16)
    mv = mv_ref[...].astype(BF16)
    head_of_lane = _iota((rows, MEM_WIDTH), 1) >> 6
    if nb > 1:
        nk = nb * N_MEM
        own = (_iota((rows, nk), 0) // rb) == (_iota((rows, nk), 1) >> 8)
    out = jnp.zeros((rows, MEM_WIDTH), F32)
    for h in range(MEM_HEADS):
        in_head = head_of_lane == h
        qh = jnp.where(in_head, q, 0.0).astype(BF16)
        s = (_dot(qh, mk) if mem_t else _dot_nt(qh, mk)) * SCALE
        if nb > 1:
            s = jnp.where(own, s, NEG_BIG)
        e = jnp.exp(s - jnp.max(s, axis=-1, keepdims=True))
        p = (e / jnp.sum(e, axis=-1, keepdims=True)).astype(BF16)
        o = _dot_nt(p, mv) if mem_t else _dot(p, mv)
        out = jnp.where(in_head, o, out)
    return out


def _row_grid(n, rows_per_group, tm):
    nt = rows_per_group // tm
    groups = n // rows_per_group
    row = lambda w: pl.BlockSpec((tm, w), lambda g, t: (g * nt + t, 0))
    const = lambda shape: pl.BlockSpec(shape, lambda g, t: (0,) * len(shape))
    return (groups, nt), row, const


def _mem_spec(layer, groups, nb, mem_t):
    rows = MEM_WIDTH if mem_t else nb * N_MEM
    return pl.BlockSpec((rows, N_MEM if mem_t else MEM_WIDTH), lambda g, t: (layer * groups + g, 0))


def _mem_kv_kernel(mem_ref, wt_ref, k_ref, v_ref):
    kvt = _dot_nt(wt_ref[...], mem_ref[...].astype(BF16))
    k_ref[...] = kvt[:MEM_WIDTH]
    v_ref[...] = kvt[MEM_WIDTH:]


def _mem_kv(mem, wt_bf):
    nbatch = mem.shape[0] // N_MEM
    out = jax.ShapeDtypeStruct((DEPTH * nbatch * MEM_WIDTH, N_MEM), F32)
    ospec = pl.BlockSpec((MEM_WIDTH, N_MEM), lambda l, b: (l * nbatch + b, 0))
    return pl.pallas_call(
        _mem_kv_kernel,
        grid=(DEPTH, nbatch),
        in_specs=[pl.BlockSpec((N_MEM, D_MODEL), lambda l, b: (b, 0)),
                  pl.BlockSpec((None, 2 * MEM_WIDTH, D_MODEL), lambda l, b: (l, 0, 0))],
        out_specs=[ospec, ospec],
        out_shape=[out, out],
        compiler_params=_params("arbitrary", "arbitrary"),
        name="mem_kv",
    )(mem, wt_bf)


def _mixer_a_kernel(x_ref, n0_ref, n1_ref, win_ref, gv_ref, wt_ref, bsb_ref, mk_ref, mv_ref,
                    wout_ref, xo_ref, *rest, seg, nb, rb, mem_t, emit_v):
    if emit_v:
        v_ref, gate_ref = rest
    else:
        (gate_ref,) = rest
    x = x_ref[...]
    rows = x.shape[0]
    h = _rms(x, n0_ref[...]).astype(BF16)
    proj = _dot(h, win_ref[...])
    z = _gelu(proj[:, :2 * GMLP_WIDTH])
    u = z[:, :GMLP_WIDTH]
    v = _rms(z[:, GMLP_WIDTH:], gv_ref[...])
    if emit_v:
        v_ref[...] = v
    vb = v.astype(BF16)
    r = _iota((CHUNK, CHUNK), 0)
    c = _iota((CHUNK, CHUNK), 1)
    keep = ((r // seg) == (c // seg)) & (c <= r)
    for g in range(GMLP_GROUPS):
        w = jnp.where(keep, wt_ref[g], 0.0).astype(BF16)
        gs = slice(g * CHUNK, (g + 1) * CHUNK)
        for ch in range(rows // CHUNK):
            rs = slice(ch * CHUNK, (ch + 1) * CHUNK)
            s = _dot(w, vb[rs, gs]) + bsb_ref[g]
            gate_ref[rs, gs] = (u[rs, gs] * s).astype(BF16)
    m_out = _mem_attention(proj[:, 2 * GMLP_WIDTH:], mk_ref, mv_ref, nb, rb, mem_t)
    mix = (_dot(gate_ref[...], wout_ref[:GMLP_WIDTH, :])
           + _dot(m_out.astype(BF16), wout_ref[GMLP_WIDTH:, :]))
    xo_ref[...] = x + _rms(mix, n1_ref[...])


def _mixer_a(x, n0, n1, win_bf, gv, wt, bsb, mk, mv, layer, wout_bf, *, tm, seg, nb, rb, mem_t,
             emit_v):
    n = x.shape[0]
    grid, row, const = _row_grid(n, rb * nb, tm)
    mem_spec = _mem_spec(layer, grid[0], nb, mem_t)
    out_shape = [jax.ShapeDtypeStruct((n, D_MODEL), F32)]
    out_specs = [row(D_MODEL)]
    if emit_v:
        out_shape.append(jax.ShapeDtypeStruct((n, GMLP_WIDTH), F32))
        out_specs.append(row(GMLP_WIDTH))
    return pl.pallas_call(
        functools.partial(_mixer_a_kernel, seg=seg, nb=nb, rb=rb, mem_t=mem_t, emit_v=emit_v),
        grid=grid,
        in_specs=[row(D_MODEL), const((1, D_MODEL)), const((1, D_MODEL)),
                  const((D_MODEL, 2 * GMLP_WIDTH + MEM_WIDTH)), const((1, GMLP_WIDTH)),
                  const((GMLP_GROUPS, CHUNK, CHUNK)), const((GMLP_GROUPS, CHUNK, CHUNK)),
                  mem_spec, mem_spec, const((D_MODEL, D_MODEL))],
        out_specs=out_specs,
        out_shape=out_shape,
        scratch_shapes=[pltpu.VMEM((tm, GMLP_WIDTH), BF16)],
        compiler_params=_params("arbitrary", "arbitrary"),
        name="mixer_a",
    )(x, n0, n1, win_bf, gv, wt, bsb, mk, mv, wout_bf)


def _conv_ffn_kernel(*refs, has_state, nf):
    if has_state:
        (x_ref, n2_ref, n3_ref, wa_ref, wg_ref, cwa_ref, cwg_ref, cba_ref, cbg_ref, wd_ref,
         s1a_ref, s1g_ref, s2a_ref, s2g_ref,
         xo_ref, ta_ref, tg_ref, h_ref, acc_ref, buf_ref) = refs
    else:
        (x_ref, xh_ref, n2_ref, n3_ref, wa_ref, wg_ref, cwa_ref, cwg_ref, cba_ref, cbg_ref,
         wd_ref, xo_ref, ta_ref, tg_ref, h_ref, hh_ref, acc_ref, buf_ref) = refs
    t = pl.program_id(1)
    f = pl.program_id(2)
    rows = x_ref.shape[0]

    @pl.when(f == 0)
    def _():
        h_ref[...] = _rms(x_ref[...], n2_ref[...]).astype(BF16)
        acc_ref[...] = jnp.zeros_like(acc_ref)
        if not has_state:
            hh_ref[...] = _rms(xh_ref[...], n2_ref[...]).astype(BF16)

    def conv_half(w_ref, cw_ref, cb_ref, tail_ref, s1_ref, s2_ref):
        up = _dot(h_ref[...], w_ref[...])
        tail_ref[...] = up[rows - tail_ref.shape[0]:, :]
        buf_ref[SUBLANES:, :] = up
        if has_state:
            buf_ref[:SUBLANES, :] = jnp.zeros((SUBLANES, up.shape[1]), F32)
        else:
            halo = _dot(hh_ref[...], w_ref[...])
            buf_ref[:SUBLANES, :] = jnp.where(t > 0, halo, 0.0)
        m1 = buf_ref[SUBLANES - 1:SUBLANES - 1 + rows, :]
        m2 = buf_ref[SUBLANES - 2:SUBLANES - 2 + rows, :]
        if has_state:
            pos = _iota(up.shape, 0) & (SUBLANES - 1)
            m1 = jnp.where(pos == 0, s1_ref[...], m1)
            m2 = jnp.where(pos < 2, s2_ref[...], m2)
        cw = cw_ref[...]
        return cb_ref[...] + cw[0:1] * m2 + cw[1:2] * m1 + cw[2:3] * up

    if has_state:
        ca = conv_half(wa_ref, cwa_ref, cba_ref, ta_ref, s1a_ref, s2a_ref)
        cg = conv_half(wg_ref, cwg_ref, cbg_ref, tg_ref, s1g_ref, s2g_ref)
    else:
        ca = conv_half(wa_ref, cwa_ref, cba_ref, ta_ref, None, None)
        cg = conv_half(wg_ref, cwg_ref, cbg_ref, tg_ref, None, None)
    y = (_gelu(ca) * cg).astype(BF16)
    acc_ref[...] += _dot(y, wd_ref[...])

    @pl.when(f == nf - 1)
    def _():
        xo_ref[...] = x_ref[...] + _rms(acc_ref[...], n3_ref[...])


def _conv_ffn(x, n2, n3, wup_bf, cw, cb, wdown_bf, *, nb, nt, tm, state=None):
    n = x.shape[0]
    tf = FFN_COLS
    nf = D_FF // tf
    has_state = state is not None
    xrow = pl.BlockSpec((tm, D_MODEL), lambda b, t, f: (b * nt + t, 0))
    vec = pl.BlockSpec((1, D_MODEL), lambda b, t, f: (0, 0))
    col_a = lambda r: pl.BlockSpec((r, tf), lambda b, t, f: (0, f))
    col_g = lambda r: pl.BlockSpec((r, tf), lambda b, t, f: (0, nf + f))
    in_specs = [xrow]
    args = [x]
    if not has_state:
        per8 = tm // SUBLANES
        in_specs.append(pl.BlockSpec(
            (SUBLANES, D_MODEL), lambda b, t, f: (jnp.maximum((b * nt + t) * per8 - 1, 0), 0)))
        args.append(x)
    in_specs += [vec, vec, col_a(D_MODEL), col_g(D_MODEL), col_a(CONV_W), col_g(CONV_W),
                 col_a(1), col_g(1), pl.BlockSpec((tf, D_MODEL), lambda b, t, f: (f, 0))]
    args += [n2, n3, wup_bf, wup_bf, cw, cw, cb, cb, wdown_bf]
    scratch = [pltpu.VMEM((tm, D_MODEL), BF16)]
    if has_state:
        s1, s2 = state
        in_specs += [col_a(tm), col_g(tm), col_a(tm), col_g(tm)]
        args += [s1, s1, s2, s2]
    else:
        scratch.append(pltpu.VMEM((SUBLANES, D_MODEL), BF16))
    scratch += [pltpu.VMEM((tm, D_MODEL), F32), pltpu.VMEM((tm + SUBLANES, tf), F32)]
    tail_rows = tm if has_state else SUBLANES
    tail = jax.ShapeDtypeStruct((nb * nt * tail_rows, D_FF), F32)
    tail_spec = pl.BlockSpec((tail_rows, tf), lambda b, t, f: (b * nt + t, f))
    return pl.pallas_call(
        functools.partial(_conv_ffn_kernel, has_state=has_state, nf=nf),
        grid=(nb, nt, nf),
        in_specs=in_specs,
        out_specs=[xrow, tail_spec, tail_spec],
        out_shape=[jax.ShapeDtypeStruct((n, D_MODEL), F32), tail, tail],
        scratch_shapes=scratch,
        compiler_params=_params("arbitrary", "arbitrary", "arbitrary"),
        name="conv_ffn",
    )(*args)


def _kvq_prompt_kernel(x_ref, kvn_ref, n0_ref, wkvt_ref, wv_ref, win_ref, mk_ref, mv_ref,
                       kt_ref, vt_ref, ktb_ref, ve_ref, vo_ref, qe_ref, qo_ref, m_ref):
    x = x_ref[...]
    hkv = _rms(x, kvn_ref[...]).astype(BF16)
    kvt = _dot_nt(wkvt_ref[...], hkv)
    kt_ref[...] = kvt[:SB_WIDTH]
    vt_ref[...] = kvt[SB_WIDTH:]
    ktb_ref[...] = kvt[:SB_WIDTH].astype(BF16)
    v = _dot(hkv, wv_ref[...])
    proj = _dot(_rms(x, n0_ref[...]).astype(BF16), win_ref[...])
    q = proj[:, :SB_WIDTH] * SCALE
    even = ((_iota(q.shape, 1) >> 6) & 1) == 0
    ve_ref[...] = jnp.where(even, v, 0.0).astype(BF16)
    vo_ref[...] = jnp.where(even, 0.0, v).astype(BF16)
    qe_ref[...] = jnp.where(even, q, 0.0).astype(BF16)
    qo_ref[...] = jnp.where(even, 0.0, q).astype(BF16)
    m_ref[...] = _mem_attention(proj[:, SB_WIDTH:], mk_ref, mv_ref, 1, 0, True).astype(BF16)


def _kvq_prompt(x, kvn, n0, wkvt_bf, wv_bf, win_bf, mk, mv, layer, *, tm, seq):
    n = x.shape[0]
    grid, row, const = _row_grid(n, seq, tm)
    mem_spec = _mem_spec(layer, grid[0], 1, True)
    tspec = pl.BlockSpec((SB_WIDTH, tm), lambda g, t: (g, t))
    tshape = lambda dt: jax.ShapeDtypeStruct((grid[0] * SB_WIDTH, seq), dt)
    sds = lambda w: jax.ShapeDtypeStruct((n, w), BF16)
    return pl.pallas_call(
        _kvq_prompt_kernel,
        grid=grid,
        in_specs=[row(D_MODEL), const((1, D_MODEL)), const((1, D_MODEL)),
                  const((2 * SB_WIDTH, D_MODEL)), const((D_MODEL, SB_WIDTH)),
                  const((D_MODEL, D_MODEL)), mem_spec, mem_spec],
        out_specs=[tspec, tspec, tspec] + [row(SB_WIDTH)] * 4 + [row(MEM_WIDTH)],
        out_shape=[tshape(F32), tshape(F32), tshape(BF16)] + [sds(SB_WIDTH)] * 4 + [sds(MEM_WIDTH)],
        compiler_params=_params("arbitrary", "arbitrary"),
        name="kvq_prompt",
    )(x, kvn, n0, wkvt_bf, wv_bf, win_bf, mk, mv)


def _kvq_sample_kernel(x_ref, kvn_ref, n0_ref, wkv_ref, win_ref, mk_ref, mv_ref,
                       k_ref, v_ref, q_ref, m_ref, *, nb, rb):
    x = x_ref[...]
    kv = _dot(_rms(x, kvn_ref[...]).astype(BF16), wkv_ref[...])
    k_ref[...] = kv[:, :SB_WIDTH]
    v_ref[...] = kv[:, SB_WIDTH:]
    proj = _dot(_rms(x, n0_ref[...]).astype(BF16), win_ref[...])
    q_ref[...] = proj[:, :SB_WIDTH] * SCALE
    m_ref[...] = _mem_attention(proj[:, SB_WIDTH:], mk_ref, mv_ref, nb, rb, False).astype(BF16)


def _kvq_sample(x, kvn, n0, wkv_bf, win_bf, mk, mv, layer, *, tm, nb, rb):
    n = x.shape[0]
    grid, row, const = _row_grid(n, rb * nb, tm)
    mem_spec = _mem_spec(layer, grid[0], nb, False)
    sds = lambda w, dt: jax.ShapeDtypeStruct((n, w), dt)
    return pl.pallas_call(
        functools.partial(_kvq_sample_kernel, nb=nb, rb=rb),
        grid=grid,
        in_specs=[row(D_MODEL), const((1, D_MODEL)), const((1, D_MODEL)),
                  const((D_MODEL, 2 * SB_WIDTH)), const((D_MODEL, D_MODEL)), mem_spec, mem_spec],
        out_specs=[row(SB_WIDTH)] * 3 + [row(MEM_WIDTH)],
        out_shape=[sds(SB_WIDTH, F32)] * 3 + [sds(MEM_WIDTH, BF16)],
        compiler_params=_params("arbitrary", "arbitrary"),
        name="kvq_sample",
    )(x, kvn, n0, wkv_bf, win_bf, mk, mv)


def _sb_block(z, carry, scan_ref, mask):
    soft = jnp.maximum(z, 0.0) + jnp.log(1.0 + jnp.exp(-jnp.abs(z)))
    log_beta = z - soft
    log_1m = -soft
    if mask is not None:
        log_1m = jnp.where(mask, log_1m, 0.0)
    hi = log_1m.astype(BF16)
    lo = (log_1m - hi.astype(F32)).astype(BF16)
    scan = _dot(jnp.concatenate([hi, lo], axis=1), scan_ref[...])
    excl = scan[:, :LANES]
    total = scan[:, LANES:]
    a = jnp.exp(log_beta + excl + carry)
    if mask is not None:
        a = jnp.where(mask, a, 0.0)
    return a, carry + total


def _sb_prompt_kernel(bias_ref, qe_ref, qo_ref, kt_ref, ve_ref, vo_ref, scan_ref, o_ref):
    i = pl.program_id(1)
    rows = qe_ref.shape[0]
    top = _iota((2 * rows, LANES), 0) < rows
    qpos = _iota((2 * rows, LANES), 0) & (rows - 1)
    causal = _iota((2 * rows, LANES), 1) < qpos
    for p in range(HEAD_PAIRS):
        ls = slice(p * LANES, (p + 1) * LANES)
        q2 = jnp.concatenate([qe_ref[:, ls], qo_ref[:, ls]], axis=0)
        bias = jnp.where(top, bias_ref[2 * p], bias_ref[2 * p + 1])

        def step(j, state, mask):
            carry, acc = state
            ks = pl.ds(pl.multiple_of(j * rows, rows), rows)
            z = _dot(q2, kt_ref[ls, ks]) + bias
            a, carry = _sb_block(z, carry, scan_ref, mask)
            a2 = jnp.concatenate([a[:rows], a[rows:]], axis=1).astype(BF16)
            v2 = jnp.concatenate([ve_ref[ks, ls], vo_ref[ks, ls]], axis=0)
            return carry, acc + _dot(a2, v2)

        state = (jnp.zeros((2 * rows, LANES), F32), jnp.zeros((rows, LANES), F32))
        state = step(i, state, causal)
        state = lax.fori_loop(0, i, lambda it, s: step(i - 1 - it, s, None), state)
        o_ref[:, ls] = state[1].astype(BF16)


def _sb_prompt(bias, qe, qo, ktb, ve, vo, scan, *, nbatch, seq):
    n = qe.shape[0]
    qb = CHUNK
    nq = seq // qb
    qspec = pl.BlockSpec((qb, SB_WIDTH), lambda b, i: (b * nq + i, 0))
    vspec = pl.BlockSpec((seq, SB_WIDTH), lambda b, i: (b, 0))
    return pl.pallas_call(
        _sb_prompt_kernel,
        grid=(nbatch, nq),
        in_specs=[pl.BlockSpec(memory_space=pltpu.SMEM), qspec, qspec,
                  pl.BlockSpec((SB_WIDTH, seq), lambda b, i: (b, 0)), vspec, vspec,
                  pl.BlockSpec((2 * LANES, 2 * LANES), lambda b, i: (0, 0))],
        out_specs=qspec,
        out_shape=jax.ShapeDtypeStruct((n, SB_WIDTH), BF16),
        compiler_params=_params("arbitrary", "arbitrary"),
        name="sb_prompt",
    )(bias, qe, qo, ktb, ve, vo, scan)


def _sb_sample_kernel(pt_ref, q_ref, kn_ref, vn_ref, bias_ref, scan_ref, *rest, pages, nsteps):
    kt_refs = rest[:pages]
    vt_refs = rest[pages:2 * pages]
    o_ref, qbd_ref, carry_ref, acc_ref, pad_ref = rest[2 * pages:]
    g = pl.program_id(1)
    nq = q_ref.shape[0]
    nr = SB_HEADS * nq

    def sweep(k, v, mask, feature_major):
        qbd = qbd_ref[...]
        z = (_dot(qbd, k) if feature_major else _dot_nt(qbd, k)) + bias_ref[...]
        a, carry = _sb_block(z, carry_ref[...], scan_ref, mask)
        carry_ref[...] = carry
        a = a.astype(BF16)
        acc_ref[...] += _dot_nt(a, v) if feature_major else _dot(a, v)

    @pl.when(g == 0)
    def _():
        q = q_ref[...]
        head_of_lane = _iota(q.shape, 1) >> 6
        qbd_ref[...] = jnp.concatenate(
            [jnp.where(head_of_lane == h, q, 0.0) for h in range(SB_HEADS)], axis=0).astype(BF16)
        carry_ref[...] = jnp.zeros_like(carry_ref)
        acc_ref[...] = jnp.zeros_like(acc_ref)
        mask = _iota((nr, LANES), 1) < (_iota((nr, LANES), 0) & (nq - 1))
        pad_ref[...] = jnp.zeros_like(pad_ref)
        pad_ref[:nq, :] = kn_ref[...]
        knew = pad_ref[...].astype(BF16)
        pad_ref[:nq, :] = vn_ref[...]
        sweep(knew, pad_ref[...].astype(BF16), mask, False)

    for r in range(pages):
        sweep(kt_refs[r][...].astype(BF16), vt_refs[r][...].astype(BF16), None, True)

    @pl.when(g == nsteps - 1)
    def _():
        acc = acc_ref[...]
        head_of_lane = _iota((nq, SB_WIDTH), 1) >> 6
        out = jnp.zeros((nq, SB_WIDTH), F32)
        for h in range(SB_HEADS):
            out = jnp.where(head_of_lane == h, acc[h * nq:(h + 1) * nq, :], out)
        o_ref[...] = out


def _sb_sample(page_table, q, kn, vn, bias_rows, scan, ktpages, vtpages):
    nseq, npages = page_table.shape
    nq = q.shape[0] // nseq
    pages = PAGES_PER_STEP
    nsteps = npages // pages
    nr = SB_HEADS * nq
    row = pl.BlockSpec((nq, SB_WIDTH), lambda b, g, pt: (b, 0))

    def page_spec(r):
        return pl.BlockSpec(
            (None, SB_WIDTH, PAGE_SIZE),
            lambda b, g, pt: (pt[b * npages + npages - 1 - (g * pages + r)], 0, 0))

    grid_spec = pltpu.PrefetchScalarGridSpec(
        num_scalar_prefetch=1,
        grid=(nseq, nsteps),
        in_specs=[row, row, row,
                  pl.BlockSpec((nr, LANES), lambda b, g, pt: (0, 0)),
                  pl.BlockSpec((2 * LANES, 2 * LANES), lambda b, g, pt: (0, 0))]
                 + [page_spec(r) for r in range(pages)] + [page_spec(r) for r in range(pages)],
        out_specs=row,
        scratch_shapes=[pltpu.VMEM((nr, SB_WIDTH), BF16), pltpu.VMEM((nr, LANES), F32),
                        pltpu.VMEM((nr, SB_WIDTH), F32), pltpu.VMEM((PAGE_SIZE, SB_WIDTH), F32)],
    )
    return pl.pallas_call(
        functools.partial(_sb_sample_kernel, pages=pages, nsteps=nsteps),
        grid_spec=grid_spec,
        out_shape=jax.ShapeDtypeStruct(q.shape, F32),
        compiler_params=_params("arbitrary", "arbitrary"),
        name="sb_sample",
    )(page_table.reshape(-1), q, kn, vn, bias_rows, scan,
      *([ktpages] * pages), *([vtpages] * pages))


def _mixer_b_out_kernel(x_ref, sb_ref, m_ref, w_ref, n1_ref, xo_ref):
    mix = (_dot(sb_ref[...].astype(BF16), w_ref[:SB_WIDTH, :])
           + _dot(m_ref[...], w_ref[SB_WIDTH:, :]))
    xo_ref[...] = x_ref[...] + _rms(mix, n1_ref[...])


def _mixer_b_out(x, sb, m, w_bf, n1, *, tm):
    n = x.shape[0]
    row = lambda w: pl.BlockSpec((tm, w), lambda t: (t, 0))
    return pl.pallas_call(
        _mixer_b_out_kernel,
        grid=(n // tm,),
        in_specs=[row(D_MODEL), row(SB_WIDTH), row(MEM_WIDTH), _full((D_MODEL, D_MODEL)),
                  _full((1, D_MODEL))],
        out_specs=row(D_MODEL),
        out_shape=jax.ShapeDtypeStruct((n, D_MODEL), F32),
        compiler_params=_params("arbitrary"),
        name="mixer_b_out",
    )(x, sb, m, w_bf, n1)


def _scan_matrix():
    j = jnp.arange(2 * LANES)[:, None] % LANES
    k = jnp.arange(2 * LANES)[None, :]
    return jnp.where(k < LANES, j > k, True).astype(BF16)


def _trunk(x, mk, mv, w, *, nbatch, seq, sample):
    is_sample = sample is not None
    if is_sample:
        tm, seg, nb = CHUNK, seq, CHUNK // seq
    else:
        tm, seg, nb = MIXER_ROWS, CHUNK, 1
    norms = w["norms"]
    nrm = lambda i, j: norms[i, j][None, :]
    seg_w = jnp.tile(w["gmlp_w_s"][0][:, :seg, :seg], (1, CHUNK // seg, CHUNK // seg))
    seg_b = jnp.broadcast_to(
        jnp.tile(w["gmlp_b_s"][0][:, :seg], (1, CHUNK // seg))[:, :, None],
        (GMLP_GROUPS, CHUNK, CHUNK))
    outs = _mixer_a(x, nrm(0, 0), nrm(0, 1), w["w_in_a"], w["gmlp_g_v"], seg_w, seg_b, mk, mv, 0,
                    w["w_out_a"], tm=tm, seg=seg, nb=nb, rb=seq, mem_t=not is_sample,
                    emit_v=is_sample)
    x = outs[0]
    v_rows = outs[1] if is_sample else None

    def ffn(x, layer):
        args = (x, nrm(layer, 2), nrm(layer, 3), w["w_up"][layer], w["conv_w"][layer],
                w["conv_b"][layer][None, :], w["w_down"][layer])
        if is_sample:
            st = sample["state_conv"][layer]
            s1 = jnp.pad(st[:, 1:2], ((0, 0), (0, seq - 1), (0, 0)))
            s2 = jnp.pad(st, ((0, 0), (0, seq - 2), (0, 0)))
            n = nbatch * seq
            xo, ta, tg = _conv_ffn(*args, nb=1, nt=1, tm=n,
                                   state=(s1.reshape(n, -1), s2.reshape(n, -1)))
            rows = jnp.concatenate([ta, tg], axis=-1).reshape(nbatch, seq, 2 * D_FF)
            return xo, rows[:, seq - 2:]
        nt = seq // FFN_ROWS
        xo, ta, tg = _conv_ffn(*args, nb=nbatch, nt=nt, tm=FFN_ROWS)
        rows = jnp.concatenate([ta, tg], axis=-1).reshape(nbatch, nt, SUBLANES, 2 * D_FF)
        return xo, rows[:, nt - 1, SUBLANES - 2:]

    x, conv0 = ffn(x, 0)
    scan = _scan_matrix()
    if is_sample:
        k, v, q, m = _kvq_sample(x, w["kv_norm"], nrm(1, 0), w["w_kv"], w["w_in_b"], mk, mv, 1,
                                 tm=tm, nb=nb, rb=seq)
        bias_rows = jnp.broadcast_to(jnp.repeat(w["sb_bias"], seq)[:, None], (SB_HEADS * seq, LANES))
        sb = _sb_sample(sample["page_table"], q, k, v, bias_rows, scan,
                        sample["kt_pages"], sample["vt_pages"])
    else:
        k, v, ktb, ve, vo, qe, qo, m = _kvq_prompt(
            x, w["kv_norm"], nrm(1, 0), w["w_kv"].T, w["w_kv"][:, SB_WIDTH:], w["w_in_b"],
            mk, mv, 1, tm=tm, seq=seq)
        sb = _sb_prompt(w["sb_bias"], qe, qo, ktb, ve, vo, scan, nbatch=nbatch, seq=seq)
    x = _mixer_b_out(x, sb, m, w["w_out_b"], nrm(1, 1), tm=min(x.shape[0], MIXER_ROWS))
    x, conv1 = ffn(x, 1)
    return x, k, v, jnp.stack([conv0, conv1]), v_rows


def kernel(x_prompt, x_sample, mem_prompt, cache_k_pages, cache_v_pages, page_table, cache_mem_k, cache_mem_v, state_conv, norms, w_in_a, gmlp_g_v, gmlp_w_s, gmlp_b_s, w_out_a, kv_norm, w_kv, w_in_b, w_out_b, sb_bias, w_mem_kv, w_up, conv_w, conv_b, w_down):
    batch, seq, d = x_prompt.shape
    dec_batch, dec_seq, _ = x_sample.shape
    w = {
        "norms": norms,
        "w_in_a": w_in_a[0].astype(BF16), "gmlp_g_v": gmlp_g_v, "gmlp_w_s": gmlp_w_s,
        "gmlp_b_s": gmlp_b_s, "w_out_a": w_out_a[0].astype(BF16), "kv_norm": kv_norm[None, :],
        "w_kv": w_kv.astype(BF16), "w_in_b": w_in_b[0].astype(BF16),
        "w_out_b": w_out_b[0].astype(BF16), "sb_bias": sb_bias[0],
        "w_up": w_up.astype(BF16), "conv_w": conv_w, "conv_b": conv_b,
        "w_down": w_down.astype(BF16),
    }
    mkt_p, mvt_p = _mem_kv(mem_prompt.reshape(batch * N_MEM, d),
                           w_mem_kv.astype(BF16).transpose(0, 2, 1))
    y_p, kt_p, vt_p, conv_p, _ = _trunk(x_prompt.reshape(batch * seq, d), mkt_p, mvt_p, w,
                                        nbatch=batch, seq=seq, sample=None)
    sample = {
        "state_conv": state_conv, "page_table": page_table,
        "kt_pages": cache_k_pages.transpose(0, 2, 3, 1).reshape(-1, SB_WIDTH, PAGE_SIZE),
        "vt_pages": cache_v_pages.transpose(0, 2, 3, 1).reshape(-1, SB_WIDTH, PAGE_SIZE),
    }
    y_s, k_s, v_s, conv_s, v_rows = _trunk(
        x_sample.reshape(dec_batch * dec_seq, d),
        cache_mem_k.reshape(-1, MEM_WIDTH), cache_mem_v.reshape(-1, MEM_WIDTH), w,
        nbatch=dec_batch, seq=dec_seq, sample=sample)
    to_heads = lambda t: t.reshape(batch, SB_HEADS, HEAD_DIM, seq).transpose(0, 3, 1, 2)
    to_mem = lambda t: t.reshape(DEPTH, batch, MEM_HEADS, HEAD_DIM, N_MEM).transpose(0, 1, 4, 2, 3)
    return (y_p.reshape(batch, seq, d), y_s.reshape(dec_batch, dec_seq, d),
            to_heads(kt_p), to_heads(vt_p),
            k_s.reshape(dec_batch, dec_seq, SB_HEADS, HEAD_DIM),
            v_s.reshape(dec_batch, dec_seq, SB_HEADS, HEAD_DIM),
            to_mem(mkt_p), to_mem(mvt_p),
            conv_p, conv_s, v_rows.reshape(1, dec_batch, dec_seq, GMLP_WIDTH))
```
